```python
import jax, jax.numpy as jnp
from jax import lax
import numpy as np

D_MODEL = 4096
BATCH = 2
SEQ = 8192
DEPTH = 1
DEC_BATCH = 16
DEC_SEQ = 16
PAST_LEN = 4096

CHUNK = 64
HEAD_DIM = 64
D_ATTN = D_MODEL // 2
D_GMLP = D_MODEL - D_ATTN
N_Q_HEADS = D_ATTN // HEAD_DIM
N_KV_HEADS = N_Q_HEADS // 8
Q_PER_KV = N_Q_HEADS // N_KV_HEADS
WINDOW = 128
WINDOW_CHUNKS = WINDOW // CHUNK
ROT_DIM = HEAD_DIM // 4
ROPE_THETA = 500000.0
GMLP_CHUNK = 128
GMLP_GROUPS = 8
GMLP_HEAD = D_GMLP // GMLP_GROUPS
D_IN = D_ATTN + 2 * N_KV_HEADS * HEAD_DIM + 2 * D_GMLP
N_EXPERTS = 32
TOP_K = 4
D_FF = D_MODEL
SWIGLU_LIMIT = 7.0
SWIGLU_ALPHA = 1.702
MOE_BLOCK = 128
EPS = 1e-5

kernel_name = "hybrid_streaming_gmlp_swa_moe_step"


def rms_norm(x, g):
    xf = x.astype(jnp.float32)
    y = xf * lax.rsqrt(jnp.mean(xf * xf, axis=-1, keepdims=True) + EPS)
    return (y * g.astype(jnp.float32)).astype(x.dtype)


def group_layer_norm(x, g, b, groups):
    bn, s, dw = x.shape
    xf = x.astype(jnp.float32).reshape(bn, s, groups, dw // groups)
    mu = jnp.mean(xf, axis=-1, keepdims=True)
    var = jnp.mean(jnp.square(xf - mu), axis=-1, keepdims=True)
    y = ((xf - mu) * lax.rsqrt(var + EPS)).reshape(bn, s, dw)
    return (y * g.astype(jnp.float32) + b.astype(jnp.float32)).astype(x.dtype)


def partial_rope(x, pos):
    half = ROT_DIM // 2
    inv = ROPE_THETA ** (-jnp.arange(0, ROT_DIM, 2, dtype=jnp.float32) / ROT_DIM)
    ang = pos[:, None] * inv[None, :]
    cos = jnp.cos(ang)[:, None, :]
    sin = jnp.sin(ang)[:, None, :]
    xr = x[..., :ROT_DIM].astype(jnp.float32)
    x1, x2 = xr[..., :half], xr[..., half:]
    rot = jnp.concatenate([x1 * cos - x2 * sin, x2 * cos + x1 * sin], axis=-1).astype(x.dtype)
    return jnp.concatenate([rot, x[..., ROT_DIM:]], axis=-1)


def sink_attention(q, k, v, sinks, valid):
    s = jnp.einsum('bcqkgd,bcskd->bckgqs', q, k, preferred_element_type=jnp.float32) * (HEAD_DIM ** -0.5)
    if valid is not None:
        s = jnp.where(valid[None, :, None, None, None, :], s, -jnp.inf)
    sink = sinks.astype(jnp.float32)[None, None, :, :, None, None]
    m = jnp.maximum(jnp.max(s, axis=-1, keepdims=True), sink)
    p = jnp.exp(s - m)
    p = p / (jnp.sum(p, axis=-1, keepdims=True) + jnp.exp(sink - m))
    return jnp.einsum('bckgqs,bcskd->bcqkgd', p.astype(v.dtype), v)


def chunk_causal_mask(n):
    idx = jnp.arange(n) // CHUNK
    return idx[:, None] >= idx[None, :]


def parallel_mixers(h, pos, ck, cv, w_in, b_in, sinks, ln_v_g, ln_v_b, w_spatial, b_spatial,
                    attn_out_g, gmlp_out_g, w_out, b_out):
    bn, s, _ = h.shape
    z = h @ w_in + b_in
    o1 = D_ATTN
    o2 = o1 + N_KV_HEADS * HEAD_DIM
    o3 = o2 + N_KV_HEADS * HEAD_DIM
    o4 = o3 + D_GMLP
    q = partial_rope(z[..., :o1].reshape(bn, s, N_Q_HEADS, HEAD_DIM), pos)
    k = partial_rope(z[..., o1:o2].reshape(bn, s, N_KV_HEADS, HEAD_DIM), pos)
    v = z[..., o2:o3].reshape(bn, s, N_KV_HEADS, HEAD_DIM)
    u = jax.nn.gelu(z[..., o3:o4], approximate=False)
    vg = group_layer_norm(jax.nn.gelu(z[..., o4:], approximate=False), ln_v_g, ln_v_b, GMLP_GROUPS)
    sk = sinks.reshape(N_KV_HEADS, Q_PER_KV)
    w_sp = w_spatial * chunk_causal_mask(GMLP_CHUNK)[None].astype(w_spatial.dtype)
    b_sp = b_spatial.T[:, :, None]
    if ck is None:
        n_c = s // CHUNK
        n_band = WINDOW_CHUNKS + 1
        qb = q.reshape(bn, n_c, CHUNK, N_KV_HEADS, Q_PER_KV, HEAD_DIM)

        def band(t):
            tp = jnp.pad(t, ((0, 0), (WINDOW_CHUNKS * CHUNK, 0), (0, 0), (0, 0)))
            tp = tp.reshape(bn, n_c + WINDOW_CHUNKS, CHUNK, N_KV_HEADS, HEAD_DIM)
            return jnp.concatenate([tp[:, j:j + n_c] for j in range(n_band)], axis=2)

        key_chunk = jnp.arange(n_c)[:, None] - WINDOW_CHUNKS + (jnp.arange(n_band * CHUNK) // CHUNK)[None, :]
        att = sink_attention(qb, band(k), band(v), sk, key_chunk >= 0).reshape(bn, s, D_ATTN)
        n_g = s // GMLP_CHUNK
        vb = vg.reshape(bn, n_g, GMLP_CHUNK, GMLP_GROUPS, GMLP_HEAD)
        mixed = jnp.einsum('gij,bnjgd->bnigd', w_sp, vb) + b_sp
        gm = u * mixed.reshape(bn, s, D_GMLP)
        new_k, new_v, new_vg = k[:, -WINDOW:], v[:, -WINDOW:], None
    else:
        kk = jnp.concatenate([ck, k], axis=1)[:, None]
        vv = jnp.concatenate([cv, v], axis=1)[:, None]
        qb = q.reshape(bn, 1, s, N_KV_HEADS, Q_PER_KV, HEAD_DIM)
        att = sink_attention(qb, kk, vv, sk, None).reshape(bn, s, D_ATTN)
        vb = vg.reshape(bn, s, GMLP_GROUPS, GMLP_HEAD)
        mixed = jnp.einsum('gij,bjgd->bigd', w_sp[:, :s, :s], vb) + b_sp[:s]
        gm = u * mixed.reshape(bn, s, D_GMLP)
        new_k, new_v, new_vg = k, v, vg
    merged = jnp.concatenate([rms_norm(att, attn_out_g), rms_norm(gm, gmlp_out_g)], axis=-1)
    return merged @ w_out + b_out, new_k, new_v, new_vg


def moe_ffn(h, w_router, b_router, w_gate_up, b_gate_up, w_down, b_down):
    t_n, d = h.shape
    logits = jnp.einsum('td,de->te', h, w_router, preferred_element_type=jnp.float32) + b_router.astype(jnp.float32)
    top_val, top_idx = lax.top_k(logits, TOP_K)
    gate = jax.nn.softmax(top_val, axis=-1).astype(h.dtype)
    n_assign = t_n * TOP_K
    flat_e = top_idx.reshape(n_assign)
    order = jnp.argsort(flat_e)
    sorted_e = flat_e[order]
    sorted_tok = (order // TOP_K).astype(jnp.int32)
    sorted_gate = gate.reshape(n_assign)[order]
    counts = jnp.bincount(flat_e, length=N_EXPERTS)
    padded = (counts + MOE_BLOCK - 1) // MOE_BLOCK * MOE_BLOCK
    pad_end = jnp.cumsum(padded)
    pad_start = pad_end - padded
    grp_start = jnp.cumsum(counts) - counts
    dest = pad_start[sorted_e] + jnp.arange(n_assign) - grp_start[sorted_e]
    n_blocks = n_assign // MOE_BLOCK + N_EXPERTS
    n_slots = n_blocks * MOE_BLOCK
    slot_tok = jnp.zeros((n_slots,), jnp.int32).at[dest].set(sorted_tok)
    slot_gate = jnp.zeros((n_slots,), h.dtype).at[dest].set(sorted_gate)
    block_expert = jnp.minimum(
        jnp.searchsorted(pad_end, jnp.arange(n_blocks) * MOE_BLOCK, side='right'), N_EXPERTS - 1)
    x_blocks = h[slot_tok].reshape(n_blocks, MOE_BLOCK, d)

    def expert_block(args):
        xb, e = args
        gu = xb @ w_gate_up[e] + b_gate_up[e]
        g_, u_ = jnp.split(gu, 2, axis=-1)
        g_ = jnp.minimum(g_, SWIGLU_LIMIT)
        u_ = jnp.clip(u_, -SWIGLU_LIMIT, SWIGLU_LIMIT)
        act = (u_ + 1) * g_ * jax.nn.sigmoid(SWIGLU_ALPHA * g_)
        return act @ w_down[e] + b_down[e]

    y = lax.map(expert_block, (x_blocks, block_expert)).reshape(n_slots, d)
    return jax.ops.segment_sum(y * slot_gate[:, None], slot_tok, num_segments=t_n)


def trunk_layer(x, c, pos, ck, cv, w_ada, b_ada, norm_mix_g, w_in, b_in, sinks, ln_v_g, ln_v_b,
                w_spatial, b_spatial, attn_out_g, gmlp_out_g, w_out, b_out, norm_ffn_g,
                w_router, b_router, w_gate_up, b_gate_up, w_down, b_down):
    mod = (jax.nn.silu(c) @ w_ada + b_ada)[:, None, :]
    sh1, sc1, g1, sh2, sc2, g2 = jnp.split(mod, 6, axis=-1)
    h = rms_norm(x, norm_mix_g) * (1 + sc1) + sh1
    mix, new_k, new_v, new_vg = parallel_mixers(h, pos, ck, cv, w_in, b_in, sinks, ln_v_g, ln_v_b,
                                                w_spatial, b_spatial, attn_out_g, gmlp_out_g, w_out, b_out)
    x = x + g1 * mix
    h = rms_norm(x, norm_ffn_g) * (1 + sc2) + sh2
    bn, s, d = h.shape
    ff = moe_ffn(h.reshape(bn * s, d), w_router, b_router, w_gate_up, b_gate_up, w_down, b_down)
    x = x + g2 * ff.reshape(bn, s, d)
    return x, new_k, new_v, new_vg


def setup_inputs(seed: int = 0) -> dict:
    key = jax.random.key(seed)
    ks = jax.random.split(key, 32)
    f32 = jnp.float32
    L = DEPTH
    kv_rows = min(WINDOW, PAST_LEN)

    def nrm(k, shape, scale):
        return jax.random.normal(k, shape, f32) * scale

    def gain(k, shape):
        return 1.0 + 0.02 * jax.random.normal(k, shape, f32)

    return {
        "x_prompt": nrm(ks[0], (BATCH, SEQ, D_MODEL), 1.0),
        "x_sample": nrm(ks[1], (DEC_BATCH, DEC_SEQ, D_MODEL), 1.0),
        "c_prompt": nrm(ks[2], (BATCH, D_MODEL), 1.0),
        "c_sample": nrm(ks[3], (DEC_BATCH, D_MODEL), 1.0),
        "cache_k": nrm(ks[4], (L, DEC_BATCH, kv_rows, N_KV_HEADS, HEAD_DIM), 1.0),
        "cache_v": nrm(ks[5], (L, DEC_BATCH, kv_rows, N_KV_HEADS, HEAD_DIM), 1.0),
        "w_ada": nrm(ks[6], (L, D_MODEL, 6 * D_MODEL), D_MODEL ** -0.5),
        "b_ada": nrm(ks[7], (L, 6 * D_MODEL), 0.02),
        "norm_mix_g": gain(ks[8], (L, D_MODEL)),
        "w_in": nrm(ks[9], (L, D_MODEL, D_IN), D_MODEL ** -0.5),
        "b_in": nrm(ks[10], (L, D_IN), 0.02),
        "sinks": nrm(ks[11], (L, N_Q_HEADS), 1.0),
        "ln_v_g": gain(ks[12], (L, D_GMLP)),
        "ln_v_b": nrm(ks[13], (L, D_GMLP), 0.02),
        "w_spatial": nrm(ks[14], (L, GMLP_GROUPS, GMLP_CHUNK, GMLP_CHUNK), GMLP_CHUNK ** -0.5),
        "b_spatial": gain(ks[15], (L, GMLP_GROUPS, GMLP_CHUNK)),
        "attn_out_g": gain(ks[16], (L, D_ATTN)),
        "gmlp_out_g": gain(ks[17], (L, D_GMLP)),
        "w_out": nrm(ks[18], (L, D_MODEL, D_MODEL), D_MODEL ** -0.5),
        "b_out": nrm(ks[19], (L, D_MODEL), 0.02),
        "norm_ffn_g": gain(ks[20], (L, D_MODEL)),
        "w_router": nrm(ks[21], (L, D_MODEL, N_EXPERTS), D_MODEL ** -0.5),
        "b_router": nrm(ks[22], (L, N_EXPERTS), 0.01),
        "w_gate_up": nrm(ks[23], (L, N_EXPERTS, D_MODEL, 2 * D_FF), D_MODEL ** -0.5),
        "b_gate_up": nrm(ks[24], (L, N_EXPERTS, 2 * D_FF), 0.02),
        "w_down": nrm(ks[25], (L, N_EXPERTS, D_FF, D_MODEL), D_FF ** -0.5),
        "b_down": nrm(ks[26], (L, N_EXPERTS, D_MODEL), 0.02),
        "final_norm_g": gain(ks[27], (D_MODEL,)),
    }


def reference(x_prompt, x_sample, c_prompt, c_sample, cache_k, cache_v, w_ada, b_ada, norm_mix_g,
              w_in, b_in, sinks, ln_v_g, ln_v_b, w_spatial, b_spatial, attn_out_g, gmlp_out_g,
              w_out, b_out, norm_ffn_g, w_router, b_router, w_gate_up, b_gate_up, w_down, b_down,
              final_norm_g):
    pos_p = jnp.arange(x_prompt.shape[1], dtype=jnp.float32)
    pos_s = PAST_LEN + jnp.arange(x_sample.shape[1], dtype=jnp.float32)
    xp, xs = x_prompt, x_sample
    kp_list, vp_list, ks_list, vs_list, gs_list = [], [], [], [], []
    for l in range(DEPTH):
        lw = (w_ada[l], b_ada[l], norm_mix_g[l], w_in[l], b_in[l], sinks[l], ln_v_g[l], ln_v_b[l],
              w_spatial[l], b_spatial[l], attn_out_g[l], gmlp_out_g[l], w_out[l], b_out[l],
              norm_ffn_g[l], w_router[l], b_router[l], w_gate_up[l], b_gate_up[l], w_down[l], b_down[l])
        xp, kp, vp, _ = trunk_layer(xp, c_prompt, pos_p, None, None, *lw)
        xs, ksn, vsn, gsn = trunk_layer(xs, c_sample, pos_s, cache_k[l], cache_v[l], *lw)
        kp_list.append(kp)
        vp_list.append(vp)
        ks_list.append(ksn)
        vs_list.append(vsn)
        gs_list.append(gsn)
    y_prompt = rms_norm(xp, final_norm_g)
    y_sample = rms_norm(xs, final_norm_g)
    new_k_prompt = jnp.stack(kp_list)
    new_v_prompt = jnp.stack(vp_list)
    new_k_sample = jnp.stack(ks_list)
    new_v_sample = jnp.stack(vs_list)
    new_gmlp_v_sample = jnp.stack(gs_list)
    return (y_prompt, y_sample, new_k_prompt, new_v_prompt, new_k_sample, new_v_sample, new_gmlp_v_sample)
```

```python
import functools

import numpy as np
import jax
import jax.numpy as jnp
from jax import lax
from jax.experimental import pallas as pl
from jax.experimental.pallas import tpu as pltpu

F32 = jnp.float32
BF16 = jnp.bfloat16
I32 = jnp.int32

HEAD_DIM = 64
Q_PER_KV = 8
CHUNK = 64
WINDOW = 128
ROT_DIM = 16
ROPE_THETA = 500000.0
GMLP_CHUNK = 128
GMLP_GROUPS = 8
N_EXPERTS = 32
TOP_K = 4
SWIGLU_LIMIT = 7.0
SWIGLU_ALPHA = 1.702
EPS = 1e-5
PAST_LEN = 4096

LANES = 128
VMEM_LIMIT = 56 * 1024 * 1024
ROW_CHUNK = 16
MOE_TM = 512


def _cp(*sem):
    return pltpu.CompilerParams(dimension_semantics=sem, vmem_limit_bytes=VMEM_LIMIT)


def _row_loop(n_rows, fn):
    def body(r, c):
        fn(pl.ds(pl.multiple_of(r * ROW_CHUNK, ROW_CHUNK), ROW_CHUNK))
        return c
    lax.fori_loop(0, n_rows // ROW_CHUNK, body, 0)


def _rms(x):
    return x * lax.rsqrt(jnp.mean(x * x, axis=-1, keepdims=True) + EPS)


def _gelu(x):
    return 0.5 * x * (1.0 + lax.erf(x * np.float32(np.sqrt(0.5))))


def _ada_kernel(c_ref, w_ref, b_ref, o_ref):
    c = c_ref[...]
    a = (c * jax.nn.sigmoid(c)).astype(BF16)
    o_ref[...] = jnp.dot(a, w_ref[...].astype(BF16), preferred_element_type=F32) + b_ref[...]


def _ada(c_all, w_ada, b_ada):
    r, d = c_all.shape
    n = w_ada.shape[1]
    tn = 512
    return pl.pallas_call(
        _ada_kernel,
        grid=(n // tn,),
        in_specs=[pl.BlockSpec((r, d), lambda j: (0, 0)),
                  pl.BlockSpec((d, tn), lambda j: (0, j)),
                  pl.BlockSpec((1, tn), lambda j: (0, j))],
        out_specs=pl.BlockSpec((r, tn), lambda j: (0, j)),
        out_shape=jax.ShapeDtypeStruct((r, n), F32),
        compiler_params=_cp("arbitrary"),
        name="ada_mod",
    )(c_all, w_ada, b_ada.reshape(1, n))


def _inproj_kernel(x_ref, g_ref, sc_ref, sh_ref, w_ref, b_ref, z_ref, h_ref):
    per_row = sc_ref.shape[0] != 1

    @pl.when(pl.program_id(1) == 0)
    def _():
        def step(rows):
            sc = sc_ref[rows, :] if per_row else sc_ref[...]
            sh = sh_ref[rows, :] if per_row else sh_ref[...]
            y = _rms(x_ref[rows, :]) * g_ref[...]
            h_ref[rows, :] = (y * (1.0 + sc) + sh).astype(BF16)
        _row_loop(x_ref.shape[0], step)

    z_ref[...] = jnp.dot(h_ref[...], w_ref[...], preferred_element_type=F32) + b_ref[...]


def _mod_spec(tm, width, tiles_per_mod, per_row, col=False):
    if per_row:
        return pl.BlockSpec((tm, width), lambda m, n: (m, n if col else 0))
    return pl.BlockSpec((None, 1, width), lambda m, n: (m // tiles_per_mod, 0, n if col else 0))


def _mod_spec_1d(tm, width, tiles_per_mod, per_row):
    if per_row:
        return pl.BlockSpec((tm, width), lambda m: (m, 0))
    return pl.BlockSpec((None, 1, width), lambda m: (m // tiles_per_mod, 0, 0))


def _inproj(x, g, sc, sh, w_bf, b, tm, tiles_per_mod, per_row):
    t, d = x.shape
    n = w_bf.shape[1]
    tn = d // 8
    return pl.pallas_call(
        _inproj_kernel,
        grid=(t // tm, n // tn),
        in_specs=[pl.BlockSpec((tm, d), lambda m, j: (m, 0)),
                  pl.BlockSpec((1, d), lambda m, j: (0, 0)),
                  _mod_spec(tm, d, tiles_per_mod, per_row),
                  _mod_spec(tm, d, tiles_per_mod, per_row),
                  pl.BlockSpec((d, tn), lambda m, j: (0, j)),
                  pl.BlockSpec((1, tn), lambda m, j: (0, j))],
        out_specs=pl.BlockSpec((tm, tn), lambda m, j: (m, j)),
        out_shape=jax.ShapeDtypeStruct((t, n), F32),
        scratch_shapes=[pltpu.VMEM((tm, d), BF16)],
        compiler_params=_cp("arbitrary", "arbitrary"),
        name="inproj",
    )(x, g, sc, sh, w_bf, b)


def _rope(x, tab_ref):
    outs = []
    for j in range(x.shape[1] // LANES):
        xb = x[:, j * LANES:(j + 1) * LANES]
        outs.append(xb * tab_ref[0]
                    + pltpu.roll(xb, LANES - ROT_DIM // 2, 1) * tab_ref[1]
                    + pltpu.roll(xb, ROT_DIM // 2, 1) * tab_ref[2])
    return outs[0] if len(outs) == 1 else jnp.concatenate(outs, axis=1)


def _attn_kernel(sink_ref, q_ref, kc_ref, vc_ref, kp_ref, vp_ref, tc_ref, tp_ref, mp_ref, mc_ref, g_ref,
                 o_ref, kr_ref, att_ref, *, n_heads, rope_prev, first_has_no_prev):
    q = (_rope(q_ref[...], tc_ref) * np.float32(HEAD_DIM ** -0.5)).astype(BF16)
    kc = _rope(kc_ref[...], tc_ref)
    kr_ref[...] = kc
    kp = _rope(kp_ref[...], tp_ref) if rope_prev else kp_ref[...]
    kcb, kpb = kc.astype(BF16), kp.astype(BF16)
    vcb, vpb = vc_ref[...].astype(BF16), vp_ref[...].astype(BF16)
    mask_p = mp_ref[...] > 0.5
    if first_has_no_prev:
        mask_p = jnp.where(pl.program_id(1) > 0, mp_ref[...], 0.0) > 0.5
    mask_c = mc_ref[...] > 0.5
    dn = (((1,), (1,)), ((), ()))
    for h in range(n_heads):
        kv = h // Q_PER_KV
        hs = slice(h * HEAD_DIM, (h + 1) * HEAD_DIM)
        ks = slice(kv * HEAD_DIM, (kv + 1) * HEAD_DIM)
        qh = q[:, hs]
        s1 = jnp.where(mask_p, lax.dot_general(qh, kpb[:, ks], dn, preferred_element_type=F32), -jnp.inf)
        s2 = jnp.where(mask_c, lax.dot_general(qh, kcb[:, ks], dn, preferred_element_type=F32), -jnp.inf)
        sink = sink_ref[h]
        m = jnp.maximum(jnp.maximum(jnp.max(s1, axis=-1, keepdims=True),
                                    jnp.max(s2, axis=-1, keepdims=True)), sink)
        p1 = jnp.exp(s1 - m)
        p2 = jnp.exp(s2 - m)
        den = (jnp.sum(p1, axis=-1, keepdims=True) + jnp.sum(p2, axis=-1, keepdims=True)
               + jnp.exp(sink - m))
        r = 1.0 / den
        att_ref[:, hs] = (jnp.dot((p1 * r).astype(BF16), vpb[:, ks], preferred_element_type=F32)
                          + jnp.dot((p2 * r).astype(BF16), vcb[:, ks], preferred_element_type=F32))
    o_ref[...] = (_rms(att_ref[...]) * g_ref[...]).astype(BF16)


def _attention(z, kprev, vprev, tab, mask_p, mask_c, sinks, g, *, n_batch, qb, pb, d_attn,
               prev_from_z, rope_prev):
    t = z.shape[0]
    kw = d_attn // Q_PER_KV
    nq = t // n_batch // qb
    kcol = (z.shape[1] - 2 * kw) // kw
    cur = lambda b, i: (b * nq + i, 0)
    if prev_from_z:
        prev_k = pl.BlockSpec((pb, kw), lambda b, i: (b * nq + jnp.maximum(i - 1, 0), kcol))
        prev_v = pl.BlockSpec((pb, kw), lambda b, i: (b * nq + jnp.maximum(i - 1, 0), kcol + 1))
        prev_t = pl.BlockSpec((3, pb, LANES), lambda b, i: (0, jnp.maximum(i - 1, 0), 0))
    else:
        prev_k = pl.BlockSpec((pb, kw), lambda b, i: (b * nq + i, 0))
        prev_v = pl.BlockSpec((pb, kw), lambda b, i: (b * nq + i, 0))
        prev_t = pl.BlockSpec((3, qb, LANES), lambda b, i: (0, i, 0))
    kern = functools.partial(_attn_kernel, n_heads=d_attn // HEAD_DIM, rope_prev=rope_prev,
                             first_has_no_prev=prev_from_z)
    return pl.pallas_call(
        kern,
        grid=(n_batch, nq),
        in_specs=[pl.BlockSpec(memory_space=pltpu.SMEM),
                  pl.BlockSpec((qb, d_attn), cur),
                  pl.BlockSpec((qb, kw), lambda b, i: (b * nq + i, kcol)),
                  pl.BlockSpec((qb, kw), lambda b, i: (b * nq + i, kcol + 1)),
                  prev_k, prev_v,
                  pl.BlockSpec((3, qb, LANES), lambda b, i: (0, i, 0)),
                  prev_t,
                  pl.BlockSpec((qb, pb), lambda b, i: (0, 0)),
                  pl.BlockSpec((qb, qb), lambda b, i: (0, 0)),
                  pl.BlockSpec((1, d_attn), lambda b, i: (0, 0))],
        out_specs=[pl.BlockSpec((qb, d_attn), cur), pl.BlockSpec((qb, kw), cur)],
        out_shape=[jax.ShapeDtypeStruct((t, d_attn), BF16), jax.ShapeDtypeStruct((t, kw), F32)],
        scratch_shapes=[pltpu.VMEM((qb, d_attn), F32)],
        compiler_params=_cp("arbitrary", "arbitrary"),
        name="sink_attention",
    )(sinks, z, z, z, kprev, vprev, tab, tab, mask_p, mask_c, g)


def _gmlp_kernel(zu_ref, zv_ref, wm_ref, bsp_ref, lng_ref, lnb_ref, og_ref, *out_and_scratch, emit_vg):
    if emit_vg:
        o_ref, vg_ref, gm_ref = out_and_scratch
    else:
        o_ref, gm_ref = out_and_scratch
    rt, dg = zu_ref.shape
    r = wm_ref.shape[1]
    gw = dg // GMLP_GROUPS
    for c in range(rt // r):
        rows = slice(c * r, (c + 1) * r)
        for g in range(GMLP_GROUPS):
            cols = slice(g * gw, (g + 1) * gw)
            a = _gelu(zv_ref[rows, cols])
            d = a - jnp.mean(a, axis=-1, keepdims=True)
            var = jnp.mean(d * d, axis=-1, keepdims=True)
            vg = d * lax.rsqrt(var + EPS) * lng_ref[:, cols] + lnb_ref[:, cols]
            if emit_vg:
                vg_ref[rows, cols] = vg
            mixed = jnp.dot(wm_ref[g], vg.astype(BF16), preferred_element_type=F32) + bsp_ref[:, cols]
            gm_ref[rows, cols] = _gelu(zu_ref[rows, cols]) * mixed
        o_ref[rows, :] = (_rms(gm_ref[rows, :]) * og_ref[...]).astype(BF16)


def _gmlp(z, wm, bsp, ln_g, ln_b, out_g, *, rt, dg, emit_vg):
    t = z.shape[0]
    r = wm.shape[1]
    out_shape = [jax.ShapeDtypeStruct((t, dg), BF16)]
    out_specs = [pl.BlockSpec((rt, dg), lambda i: (i, 0))]
    if emit_vg:
        out_shape.append(jax.ShapeDtypeStruct((t, dg), F32))
        out_specs.append(pl.BlockSpec((rt, dg), lambda i: (i, 0)))
    res = pl.pallas_call(
        functools.partial(_gmlp_kernel, emit_vg=emit_vg),
        grid=(t // rt,),
        in_specs=[pl.BlockSpec((rt, dg), lambda i: (i, 1)),
                  pl.BlockSpec((rt, dg), lambda i: (i, 2)),
                  pl.BlockSpec((GMLP_GROUPS, r, r), lambda i: (0, 0, 0)),
                  pl.BlockSpec((r, dg), lambda i: (0, 0)),
                  pl.BlockSpec((1, dg), lambda i: (0, 0)),
                  pl.BlockSpec((1, dg), lambda i: (0, 0)),
                  pl.BlockSpec((1, dg), lambda i: (0, 0))],
        out_specs=out_specs,
        out_shape=out_shape,
        scratch_shapes=[pltpu.VMEM((rt, dg), F32)],
        compiler_params=_cp("arbitrary"),
        name="gmlp",
    )(z, z, wm, bsp, ln_g, ln_b, out_g)
    return res if emit_vg else (res[0], None)


def _outproj_kernel(a_ref, m_ref, x_ref, wa_ref, wb_ref, b_ref, g1_ref, o_ref):
    acc = (jnp.dot(a_ref[...], wa_ref[...], preferred_element_type=F32)
           + jnp.dot(m_ref[...], wb_ref[...], preferred_element_type=F32) + b_ref[...])
    o_ref[...] = x_ref[...] + g1_ref[...] * acc


def _outproj(att, gm, x, w_bf, b, g1, tm, tiles_per_mod, per_row):
    t, d = x.shape
    dh = att.shape[1]
    tn = d // 4
    return pl.pallas_call(
        _outproj_kernel,
        grid=(t // tm, d // tn),
        in_specs=[pl.BlockSpec((tm, dh), lambda m, j: (m, 0)),
                  pl.BlockSpec((tm, dh), lambda m, j: (m, 0)),
                  pl.BlockSpec((tm, tn), lambda m, j: (m, j)),
                  pl.BlockSpec((dh, tn), lambda m, j: (0, j)),
                  pl.BlockSpec((dh, tn), lambda m, j: (1, j)),
                  pl.BlockSpec((1, tn), lambda m, j: (0, j)),
                  _mod_spec(tm, tn, tiles_per_mod, per_row, col=True)],
        out_specs=pl.BlockSpec((tm, tn), lambda m, j: (m, j)),
        out_shape=jax.ShapeDtypeStruct((t, d), F32),
        compiler_params=_cp("arbitrary", "arbitrary"),
        name="outproj",
    )(att, gm, x, w_bf, w_bf, b, g1)


def _router_kernel(x_ref, g_ref, sc_ref, sh_ref, whi_ref, wlo_ref, br_ref, h_ref, idx_ref, gate_ref,
                   hi_ref, lo_ref):
    per_row = sc_ref.shape[0] != 1
    tm = x_ref.shape[0]

    def step(rows):
        sc = sc_ref[rows, :] if per_row else sc_ref[...]
        sh = sh_ref[rows, :] if per_row else sh_ref[...]
        h = _rms(x_ref[rows, :]) * g_ref[...] * (1.0 + sc) + sh
        h_ref[rows, :] = h
        hi = h.astype(BF16)
        hi_ref[rows, :] = hi
        lo_ref[rows, :] = (h - hi.astype(F32)).astype(BF16)
    _row_loop(tm, step)

    logits = (jnp.dot(hi_ref[...], whi_ref[...], preferred_element_type=F32)
              + jnp.dot(lo_ref[...], whi_ref[...], preferred_element_type=F32)
              + jnp.dot(hi_ref[...], wlo_ref[...], preferred_element_type=F32)) + br_ref[...]
    lane = lax.broadcasted_iota(I32, (tm, LANES), 1).astype(F32)
    l = jnp.where(lane < N_EXPERTS, logits, -jnp.inf)
    vals, idxs = [], []
    for _ in range(TOP_K):
        m = jnp.max(l, axis=-1, keepdims=True)
        idx = jnp.min(jnp.where(l == m, lane, float(LANES)), axis=-1, keepdims=True)
        vals.append(m)
        idxs.append(idx)
        l = jnp.where(lane == idx, -jnp.inf, l)
    es = [jnp.exp(v - vals[0]) for v in vals]
    den = es[0] + es[1] + es[2] + es[3]
    idx_out = jnp.zeros((tm, LANES), I32)
    gate_out = jnp.zeros((tm, LANES), F32)
    for k in range(TOP_K):
        idx_out = jnp.where(lane == k, idxs[k].astype(I32), idx_out)
        gate_out = jnp.where(lane == k, es[k] / den, gate_out)
    idx_ref[...] = idx_out
    gate_ref[...] = gate_out


def _router(x1, g, sc, sh, whi, wlo, br, tm, tiles_per_mod, per_row):
    t, d = x1.shape
    row = lambda m: (m, 0)
    mod = _mod_spec_1d(tm, d, tiles_per_mod, per_row)
    return pl.pallas_call(
        _router_kernel,
        grid=(t // tm,),
        in_specs=[pl.BlockSpec((tm, d), row),
                  pl.BlockSpec((1, d), lambda m: (0, 0)),
                  mod, mod,
                  pl.BlockSpec((d, LANES), lambda m: (0, 0)),
                  pl.BlockSpec((d, LANES), lambda m: (0, 0)),
                  pl.BlockSpec((1, LANES), lambda m: (0, 0))],
        out_specs=[pl.BlockSpec((tm, d), row), pl.BlockSpec((tm, LANES), row),
                   pl.BlockSpec((tm, LANES), row)],
        out_shape=[jax.ShapeDtypeStruct((t, d), F32), jax.ShapeDtypeStruct((t, LANES), I32),
                   jax.ShapeDtypeStruct((t, LANES), F32)],
        scratch_shapes=[pltpu.VMEM((tm, d), BF16), pltpu.VMEM((tm, d), BF16)],
        compiler_params=_cp("arbitrary"),
        name="router_top4",
    )(x1, g, sc, sh, whi, wlo, br)


def _plan_kernel(idx_ref, dest_ref, cnt_ref, cnt_scr, carry_scr, start_scr, *, tile_rows):
    p = pl.program_id(0)
    i = pl.program_id(1)
    tp = idx_ref.shape[0]
    lane = lax.broadcasted_iota(I32, (tp, LANES), 1)
    idx = idx_ref[...]
    onehot = [idx[:, k:k + 1] == lane for k in range(TOP_K)]
    hist = sum(jnp.where(o, 1.0, 0.0) for o in onehot)
    col_sum = jnp.sum(hist, axis=0, keepdims=True)

    @pl.when((p == 0) & (i == 0))
    def _():
        cnt_scr[...] = jnp.zeros_like(cnt_scr)

    @pl.when(p == 0)
    def _():
        cnt_scr[...] += col_sum

    @pl.when((p == 1) & (i == 0))
    def _():
        cnt = cnt_scr[...]
        cnt_ref[...] = cnt
        tiles = jnp.floor((cnt + (tile_rows - 1)) * (1.0 / tile_rows))
        before = (lax.broadcasted_iota(I32, (LANES, LANES), 0)
                  < lax.broadcasted_iota(I32, (LANES, LANES), 1))
        tri = jnp.where(before, 1.0, 0.0).astype(BF16)
        start = jnp.dot(jnp.broadcast_to(tiles, (8, LANES)).astype(BF16), tri, preferred_element_type=F32)
        start_scr[...] = start[0:1] * tile_rows
        carry_scr[...] = jnp.zeros_like(carry_scr)

    @pl.when(p == 1)
    def _():
        earlier = (lax.broadcasted_iota(I32, (tp, tp), 0) > lax.broadcasted_iota(I32, (tp, tp), 1))
        low = jnp.where(earlier, 1.0, 0.0).astype(BF16)
        pos = (jnp.dot(low, hist.astype(BF16), preferred_element_type=F32)
               + carry_scr[...] + start_scr[...])
        dest = jnp.zeros((tp, LANES), I32)
        for k in range(TOP_K):
            d = jnp.sum(jnp.where(onehot[k], pos, 0.0), axis=-1, keepdims=True)
            dest = jnp.where(lane == k, d.astype(I32), dest)
        dest_ref[...] = dest
        carry_scr[...] += col_sum


def _plan(idx_all, tile_rows):
    t = idx_all.shape[0]
    tp = 256
    return pl.pallas_call(
        functools.partial(_plan_kernel, tile_rows=tile_rows),
        grid=(2, t // tp),
        in_specs=[pl.BlockSpec((tp, LANES), lambda p, i: (i, 0))],
        out_specs=[pl.BlockSpec((tp, LANES), lambda p, i: (i * p, 0)),
                   pl.BlockSpec((1, LANES), lambda p, i: (0, 0))],
        out_shape=[jax.ShapeDtypeStruct((t, LANES), I32), jax.ShapeDtypeStruct((1, LANES), F32)],
        scratch_shapes=[pltpu.VMEM((1, LANES), F32), pltpu.VMEM((1, LANES), F32),
                        pltpu.VMEM((1, LANES), F32)],
        compiler_params=_cp("arbitrary", "arbitrary"),
        name="dispatch_plan",
    )(idx_all)


def _gather_kernel(nu_ref, tok_ref, hp_ref, hs_ref, o_ref, buf_ref, sem, *, n_prompt):
    gt = o_ref.shape[0]

    @pl.when(pl.program_id(0) < nu_ref[0])
    def _():
        def copy(r):
            t = tok_ref[0, 0, r]
            dst = buf_ref.at[pl.ds(r, 1), :]
            from_prompt = t < n_prompt
            tp = jnp.where(from_prompt, t, 0)
            ts = jnp.where(from_prompt, 0, t - n_prompt)
            return (from_prompt,
                    pltpu.make_async_copy(hp_ref.at[pl.ds(tp, 1), :], dst, sem),
                    pltpu.make_async_copy(hs_ref.at[pl.ds(ts, 1), :], dst, sem))

        def issue(r, c):
            from_prompt, cp_p, cp_s = copy(r)
            pl.when(from_prompt)(cp_p.start)
            pl.when(jnp.logical_not(from_prompt))(cp_s.start)
            return c

        def wait(r, c):
            from_prompt, cp_p, cp_s = copy(r)
            pl.when(from_prompt)(cp_p.wait)
            pl.when(jnp.logical_not(from_prompt))(cp_s.wait)
            return c

        lax.fori_loop(0, gt, issue, 0)
        lax.fori_loop(0, gt, wait, 0)
        o_ref[...] = buf_ref[...].astype(BF16)


def _gather(n_used_rows, slot_tok, h_p, h_s, n_slots, gt):
    d = h_p.shape[1]
    nt = n_slots // gt
    last = lambda i, nu: (jnp.minimum(i, nu[0] - 1), 0)
    return pl.pallas_call(
        functools.partial(_gather_kernel, n_prompt=h_p.shape[0]),
        grid_spec=pltpu.PrefetchScalarGridSpec(
            num_scalar_prefetch=1,
            grid=(nt,),
            in_specs=[pl.BlockSpec((1, 1, gt), lambda i, nu: (jnp.minimum(i, nu[0] - 1), 0, 0),
                                   memory_space=pltpu.SMEM),
                      pl.BlockSpec(memory_space=pl.ANY),
                      pl.BlockSpec(memory_space=pl.ANY)],
            out_specs=pl.BlockSpec((gt, d), last),
            scratch_shapes=[pltpu.VMEM((gt, d), F32), pltpu.SemaphoreType.DMA(())]),
        out_shape=jax.ShapeDtypeStruct((n_slots, d), BF16),
        compiler_params=_cp("arbitrary"),
        name="moe_gather",
    )(n_used_rows, slot_tok.reshape(nt, 1, gt), h_p, h_s)


def _cast_weight(dst_ref, col0, w_ref):
    k, tn = w_ref.shape[1], w_ref.shape[2]
    slab = 256

    def body(r, c):
        rows = pl.ds(pl.multiple_of(r * slab, slab), slab)
        dst_ref[rows, col0:col0 + tn] = w_ref[0, rows, :].astype(BF16)
        return c
    lax.fori_loop(0, k // slab, body, 0)


def _new_expert(te_ref, i):
    return (i == 0) | (te_ref[i] != te_ref[jnp.maximum(i - 1, 0)])


def _moe_up_kernel(te_ref, nu_ref, x_ref, wg_ref, wu_ref, bg_ref, bu_ref, o_ref, w_scr):
    i = pl.program_id(1)
    tn = wg_ref.shape[2]
    used = i < nu_ref[0]

    @pl.when(used & _new_expert(te_ref, i))
    def _():
        _cast_weight(w_scr, 0, wg_ref)
        _cast_weight(w_scr, tn, wu_ref)

    @pl.when(used)
    def _():
        gu = jnp.dot(x_ref[...], w_scr[...], preferred_element_type=F32)
        g = jnp.minimum(gu[:, :tn] + bg_ref[0], SWIGLU_LIMIT)
        u = jnp.clip(gu[:, tn:] + bu_ref[0], -SWIGLU_LIMIT, SWIGLU_LIMIT)
        o_ref[...] = ((u + 1.0) * g * jax.nn.sigmoid(SWIGLU_ALPHA * g)).astype(BF16)


def _moe_down_kernel(te_ref, nu_ref, a_ref, w_ref, b_ref, o_ref, w_scr):
    i = pl.program_id(1)
    used = i < nu_ref[0]

    @pl.when(used & _new_expert(te_ref, i))
    def _():
        _cast_weight(w_scr, 0, w_ref)

    @pl.when(used)
    def _():
        o_ref[...] = jnp.dot(a_ref[...], w_scr[...], preferred_element_type=F32) + b_ref[0]


def _moe_up(tile_expert, n_used, xs, w_gate_up, b_gate_up, tm):
    s, d = xs.shape
    f = w_gate_up.shape[2] // 2
    tn = f // 16
    nb = f // tn
    tile = lambda n, i, te, nu: jnp.minimum(i, nu[0] - 1)
    expert = lambda n, i, te, nu: te[jnp.minimum(i, nu[0] - 1)]
    b3 = b_gate_up.reshape(N_EXPERTS, 1, 2 * f)
    return pl.pallas_call(
        _moe_up_kernel,
        grid_spec=pltpu.PrefetchScalarGridSpec(
            num_scalar_prefetch=2,
            grid=(nb, s // tm),
            in_specs=[pl.BlockSpec((tm, d), lambda n, i, te, nu: (tile(n, i, te, nu), 0)),
                      pl.BlockSpec((1, d, tn), lambda n, i, te, nu: (expert(n, i, te, nu), 0, n)),
                      pl.BlockSpec((1, d, tn), lambda n, i, te, nu: (expert(n, i, te, nu), 0, n + nb)),
                      pl.BlockSpec((1, 1, tn), lambda n, i, te, nu: (expert(n, i, te, nu), 0, n)),
                      pl.BlockSpec((1, 1, tn), lambda n, i, te, nu: (expert(n, i, te, nu), 0, n + nb))],
            out_specs=pl.BlockSpec((tm, tn), lambda n, i, te, nu: (tile(n, i, te, nu), n)),
            scratch_shapes=[pltpu.VMEM((d, 2 * tn), BF16)]),
        out_shape=jax.ShapeDtypeStruct((s, f), BF16),
        compiler_params=_cp("arbitrary", "arbitrary"),
        name="moe_gate_up",
    )(tile_expert, n_used, xs, w_gate_up, w_gate_up, b3, b3)


def _moe_down(tile_expert, n_used, act, w_down, b_down, tm):
    s, f = act.shape
    d = w_down.shape[2]
    tn = d // 8
    tile = lambda n, i, te, nu: jnp.minimum(i, nu[0] - 1)
    expert = lambda n, i, te, nu: te[jnp.minimum(i, nu[0] - 1)]
    return pl.pallas_call(
        _moe_down_kernel,
        grid_spec=pltpu.PrefetchScalarGridSpec(
            num_scalar_prefetch=2,
            grid=(d // tn, s // tm),
            in_specs=[pl.BlockSpec((tm, f), lambda n, i, te, nu: (tile(n, i, te, nu), 0)),
                      pl.BlockSpec((1, f, tn), lambda n, i, te, nu: (expert(n, i, te, nu), 0, n)),
                      pl.BlockSpec((1, 1, tn), lambda n, i, te, nu: (expert(n, i, te, nu), 0, n))],
            out_specs=pl.BlockSpec((tm, tn), lambda n, i, te, nu: (tile(n, i, te, nu), n)),
            scratch_shapes=[pltpu.VMEM((f, tn), BF16)]),
        out_shape=jax.ShapeDtypeStruct((s, d), F32),
        compiler_params=_cp("arbitrary", "arbitrary"),
        name="moe_down",
    )(tile_expert, n_used, act, w_down, b_down.reshape(N_EXPERTS, 1, d))


def _combine_kernel(dest_ref, gate_ref, y_ref, x_ref, g2_ref, fg_ref, o_ref, buf_ref, sem):
    ct = x_ref.shape[0]
    per_row = g2_ref.shape[0] != 1

    def copy(j):
        r = j // TOP_K
        k = j % TOP_K
        return pltpu.make_async_copy(y_ref.at[pl.ds(dest_ref[0, 0, j], 1), :],
                                     buf_ref.at[k, pl.ds(r, 1), :], sem)

    def issue(j, c):
        copy(j).start()
        return c

    def wait(j, c):
        copy(j).wait()
        return c

    lax.fori_loop(0, ct * TOP_K, issue, 0)
    lax.fori_loop(0, ct * TOP_K, wait, 0)

    def step(rows):
        gate = gate_ref[rows, :]
        ff = gate[:, 0:1] * buf_ref[0, rows, :]
        for k in range(1, TOP_K):
            ff = ff + gate[:, k:k + 1] * buf_ref[k, rows, :]
        g2 = g2_ref[rows, :] if per_row else g2_ref[...]
        o_ref[rows, :] = _rms(x_ref[rows, :] + g2 * ff) * fg_ref[...]
    _row_loop(ct, step)


def _combine(dest3, gate_all, y, x1, g2, fg, tile_off, ct, tiles_per_mod, per_row):
    t, d = x1.shape
    row = lambda i: (i, 0)
    mod = _mod_spec_1d(ct, d, tiles_per_mod, per_row)
    return pl.pallas_call(
        _combine_kernel,
        grid=(t // ct,),
        in_specs=[pl.BlockSpec((1, 1, ct * TOP_K), lambda i: (i + tile_off, 0, 0), memory_space=pltpu.SMEM),
                  pl.BlockSpec((ct, LANES), lambda i: (i + tile_off, 0)),
                  pl.BlockSpec(memory_space=pl.ANY),
                  pl.BlockSpec((ct, d), row),
                  mod,
                  pl.BlockSpec((1, d), lambda i: (0, 0))],
        out_specs=pl.BlockSpec((ct, d), row),
        out_shape=jax.ShapeDtypeStruct((t, d), F32),
        scratch_shapes=[pltpu.VMEM((TOP_K, ct, d), F32), pltpu.SemaphoreType.DMA(())],
        compiler_params=_cp("arbitrary"),
        name="moe_combine",
    )(dest3, gate_all, y, x1, g2, fg)


def _rope_tables(pos):
    half = ROT_DIM // 2
    inv = ROPE_THETA ** (-jnp.arange(0, ROT_DIM, 2, dtype=F32) / ROT_DIM)
    ang = pos[:, None] * inv[None, :]
    cos, sin = jnp.cos(ang), jnp.sin(ang)
    m = np.arange(LANES) % HEAD_DIM
    sel = m % half
    c = jnp.where(m < ROT_DIM, cos[:, sel], 1.0)
    s1 = jnp.where(m < half, -sin[:, sel], 0.0)
    s2 = jnp.where((m >= half) & (m < ROT_DIM), sin[:, sel], 0.0)
    return jnp.stack([c, s1, s2]).astype(F32)


def _trunk_to_x1(x2d, mod_rows, per_row, tm, tiles_per_mod, attn_fn, gmlp_fn, p):
    sh1, sc1, g1 = mod_rows[0], mod_rows[1], mod_rows[2]
    z = _inproj(x2d, p["norm_mix_g"], sc1, sh1, p["w_in"], p["b_in"], tm, tiles_per_mod, per_row)
    att, k_rope = attn_fn(z)
    gm, vg = gmlp_fn(z)
    x1 = _outproj(att, gm, x2d, p["w_out"], p["b_out"], g1, tm, tiles_per_mod, per_row)
    return x1, k_rope, z, vg


def kernel(x_prompt, x_sample, c_prompt, c_sample, cache_k, cache_v, w_ada, b_ada, norm_mix_g, w_in, b_in,
           sinks, ln_v_g, ln_v_b, w_spatial, b_spatial, attn_out_g, gmlp_out_g, w_out, b_out, norm_ffn_g,
           w_router, b_router, w_gate_up, b_gate_up, w_down, b_down, final_norm_g):
    nb, seq, d = x_prompt.shape
    ns, sseq, _ = x_sample.shape
    da = d // 2
    dg = d - da
    kw = da // Q_PER_KV
    n_kv = kw // HEAD_DIM
    gw = dg // GMLP_GROUPS
    tp, ts = nb * seq, ns * sseq
    l = 0

    n_mod = nb + ns
    pad = -n_mod % 16
    c_all = jnp.concatenate([c_prompt, c_sample, jnp.zeros((pad, d), F32)], axis=0)
    mod = _ada(c_all, w_ada[l], b_ada[l])
    mod_p = [m.reshape(nb, 1, d) for m in jnp.split(mod[:nb], 6, axis=-1)]
    mod_s = [jnp.repeat(m, sseq, axis=0) for m in jnp.split(mod[nb:n_mod], 6, axis=-1)]

    o1, o2, o3, o4 = da, da + kw, da + 2 * kw, da + 2 * kw + dg
    perm = np.concatenate([np.arange(0, o1), np.arange(o3, o4), np.arange(o4, o4 + dg),
                           np.arange(o1, o2), np.arange(o2, o3)])
    row1 = lambda v: v.reshape(1, -1)
    p = {
        "norm_mix_g": row1(norm_mix_g[l]),
        "w_in": w_in[l][:, perm].astype(BF16),
        "b_in": row1(b_in[l][perm]),
        "w_out": w_out[l].astype(BF16),
        "b_out": row1(b_out[l]),
    }
    cmask = (np.arange(GMLP_CHUNK)[:, None] // CHUNK) >= (np.arange(GMLP_CHUNK)[None, :] // CHUNK)
    w_sp = w_spatial[l] * jnp.asarray(cmask, F32)[None]
    wm_p = w_sp.astype(BF16)
    bsp_p = jnp.repeat(b_spatial[l].T, gw, axis=1)
    eye = jnp.eye(ns, dtype=F32)
    wm_s = jnp.einsum("ab,gij->gaibj", eye, w_sp[:, :sseq, :sseq]).reshape(
        GMLP_GROUPS, ts, ts).astype(BF16)
    bsp_s = jnp.tile(bsp_p[:sseq], (ns, 1))
    sink_v = sinks[l]
    ag, gg = row1(attn_out_g[l]), row1(gmlp_out_g[l])
    lng, lnb = row1(ln_v_g[l]), row1(ln_v_b[l])

    qb = WINDOW
    tab_p = _rope_tables(jnp.arange(seq, dtype=F32))
    qi = np.arange(qb) // CHUNK
    mask_prev = jnp.asarray(qi[None, :] >= qi[:, None], F32)
    mask_cur = jnp.asarray(qi[None, :] <= qi[:, None], F32)
    tm_p = 512
    attn_p = lambda z: _attention(z, z, z, tab_p, mask_prev, mask_cur, sink_v, ag, n_batch=nb, qb=qb, pb=qb,
                                  d_attn=da, prev_from_z=True, rope_prev=True)
    gmlp_p = lambda z: _gmlp(z, wm_p, bsp_p, lng, lnb, gg, rt=4 * GMLP_CHUNK, dg=dg, emit_vg=False)
    x1_p, k_p, z_p, _ = _trunk_to_x1(x_prompt.reshape(tp, d), mod_p, False, tm_p, seq // tm_p,
                                     attn_p, gmlp_p, p)

    tab_s = _rope_tables(jnp.tile(PAST_LEN + jnp.arange(sseq, dtype=F32), ns))
    n_cache = cache_k.shape[2]
    sid = np.arange(ts) // sseq
    mask_cache = jnp.asarray(sid[:, None] == (np.arange(ns * n_cache) // n_cache)[None, :], F32)
    mask_new = jnp.asarray(sid[:, None] == sid[None, :], F32)
    ck = cache_k[l].reshape(ns * n_cache, kw)
    cv = cache_v[l].reshape(ns * n_cache, kw)
    attn_s = lambda z: _attention(z, ck, cv, tab_s, mask_cache, mask_new, sink_v, ag, n_batch=1, qb=ts,
                                  pb=ns * n_cache, d_attn=da, prev_from_z=False, rope_prev=False)
    gmlp_s = lambda z: _gmlp(z, wm_s, bsp_s, lng, lnb, gg, rt=ts, dg=dg, emit_vg=True)
    x1_s, k_s, z_s, vg_s = _trunk_to_x1(x_sample.reshape(ts, d), mod_s, True, ts, 1, attn_s, gmlp_s, p)

    w_r = jnp.pad(w_router[l], ((0, 0), (0, LANES - N_EXPERTS)))
    whi = w_r.astype(BF16)
    wlo = (w_r - whi.astype(F32)).astype(BF16)
    br = jnp.pad(b_router[l], (0, LANES - N_EXPERTS)).reshape(1, LANES)
    fng = row1(norm_ffn_g[l])
    tm_r = 256
    h_p, idx_p, gate_p = _router(x1_p, fng, mod_p[4], mod_p[3], whi, wlo, br, tm_r, seq // tm_r, False)
    h_s, idx_s, gate_s = _router(x1_s, fng, mod_s[4], mod_s[3], whi, wlo, br, ts, 1, True)
    idx_all = jnp.concatenate([idx_p, idx_s], axis=0)
    gate_all = jnp.concatenate([gate_p, gate_s], axis=0)
    t_all = tp + ts
    n_assign = t_all * TOP_K
    tm = MOE_TM
    n_tiles = n_assign // tm + N_EXPERTS
    n_slots = n_tiles * tm

    dest_pad, cnt = _plan(idx_all, tm)
    dest = dest_pad[:, :TOP_K]
    counts = cnt[0, :N_EXPERTS].astype(I32)
    tile_end = jnp.cumsum((counts + tm - 1) // tm)
    n_used = tile_end[-1:].astype(I32)
    tile_expert = jnp.minimum(jnp.searchsorted(tile_end, jnp.arange(n_tiles), side="right"),
                              N_EXPERTS - 1).astype(I32)
    slot_tok = jnp.zeros((n_slots,), I32).at[dest.reshape(-1)].set(
        jnp.repeat(jnp.arange(t_all, dtype=I32), TOP_K))

    gt = 256
    xs = _gather(n_used * (tm // gt), slot_tok, h_p, h_s, n_slots, gt)
    act = _moe_up(tile_expert, n_used, xs, w_gate_up[l], b_gate_up[l], tm)
    y = _moe_down(tile_expert, n_used, act, w_down[l], b_down[l], tm)

    ct = 128
    dest3 = dest.reshape(t_all // ct, 1, ct * TOP_K)
    fg = row1(final_norm_g)
    y_p = _combine(dest3, gate_all, y, x1_p, mod_p[5], fg, 0, ct, seq // ct, False)
    y_s = _combine(dest3, gate_all, y, x1_s, mod_s[5], fg, tp // ct, ct, 1, True)

    vcol = z_p.shape[1] - kw
    new_k_p = k_p.reshape(nb, seq, n_kv, HEAD_DIM)[:, -WINDOW:][None]
    new_v_p = z_p[:, vcol:].reshape(nb, seq, n_kv, HEAD_DIM)[:, -WINDOW:][None]
    new_k_s = k_s.reshape(ns, sseq, n_kv, HEAD_DIM)[None]
    new_v_s = z_s[:, vcol:].reshape(ns, sseq, n_kv, HEAD_DIM)[None]
    new_vg_s = vg_s.reshape(ns, sseq, dg)[None]
    return (y_p.reshape(nb, seq, d), y_s.reshape(ns, sseq, d), new_k_p, new_v_p, new_k_s, new_v_s, new_vg_s)
```

```python
import functools

import numpy as np
import jax
import jax.numpy as jnp
from jax import lax
from jax.experimental import pallas as pl
from jax.experimental.pallas import tpu as pltpu

F32 = jnp.float32
BF16 = jnp.bfloat16
I32 = jnp.int32

HEAD_DIM = 64
Q_PER_KV = 8
CHUNK = 64
WINDOW = 128
ROT_DIM = 16
ROPE_THETA = 500000.0
GMLP_CHUNK = 128
GMLP_GROUPS = 8
N_EXPERTS = 32
TOP_K = 4
SWIGLU_LIMIT = 7.0
SWIGLU_ALPHA = 1.702
EPS = 1e-5
PAST_LEN = 4096

LANES = 128
VMEM_LIMIT = 56 * 1024 * 1024
ROW_CHUNK = 16
MOE_TM = 512


def _cp(*sem):
    return pltpu.CompilerParams(dimension_semantics=sem, vmem_limit_bytes=VMEM_LIMIT)


def _row_loop(n_rows, fn):
    def body(r, c):
        fn(pl.ds(pl.multiple_of(r * ROW_CHUNK, ROW_CHUNK), ROW_CHUNK))
        return c
    lax.fori_loop(0, n_rows // ROW_CHUNK, body, 0)


def _rms(x):
    return x * lax.rsqrt(jnp.mean(x * x, axis=-1, keepdims=True) + EPS)


def _gelu(x):
    return 0.5 * x * (1.0 + lax.erf(x * np.float32(np.sqrt(0.5))))


def _ada_kernel(c_ref, w_ref, b_ref, o_ref):
    c = c_ref[...]
    a = (c * jax.nn.sigmoid(c)).astype(BF16)
    o_ref[...] = jnp.dot(a, w_ref[...].astype(BF16), preferred_element_type=F32) + b_ref[...]


def _ada(c_all, w_ada, b_ada):
    r, d = c_all.shape
    n = w_ada.shape[1]
    tn = 512
    return pl.pallas_call(
        _ada_kernel,
        grid=(n // tn,),
        in_specs=[pl.BlockSpec((r, d), lambda j: (0, 0)),
                  pl.BlockSpec((d, tn), lambda j: (0, j)),
                  pl.BlockSpec((1, tn), lambda j: (0, j))],
        out_specs=pl.BlockSpec((r, tn), lambda j: (0, j)),
        out_shape=jax.ShapeDtypeStruct((r, n), F32),
        compiler_params=_cp("arbitrary"),
        name="ada_mod",
    )(c_all, w_ada, b_ada.reshape(1, n))


def _inproj_kernel(x_ref, g_ref, sc_ref, sh_ref, w_ref, b_ref, z_ref, h_ref):
    per_row = sc_ref.shape[0] != 1

    @pl.when(pl.program_id(1) == 0)
    def _():
        def step(rows):
            sc = sc_ref[rows, :] if per_row else sc_ref[...]
            sh = sh_ref[rows, :] if per_row else sh_ref[...]
            y = _rms(x_ref[rows, :]) * g_ref[...]
            h_ref[rows, :] = (y * (1.0 + sc) + sh).astype(BF16)
        _row_loop(x_ref.shape[0], step)

    z_ref[...] = jnp.dot(h_ref[...], w_ref[...], preferred_element_type=F32) + b_ref[...]


def _mod_spec(tm, width, tiles_per_mod, per_row, col=False):
    if per_row:
        return pl.BlockSpec((tm, width), lambda m, n: (m, n if col else 0))
    return pl.BlockSpec((None, 1, width), lambda m, n: (m // tiles_per_mod, 0, n if col else 0))


def _mod_spec_1d(tm, width, tiles_per_mod, per_row):
    if per_row:
        return pl.BlockSpec((tm, width), lambda m: (m, 0))
    return pl.BlockSpec((None, 1, width), lambda m: (m // tiles_per_mod, 0, 0))


def _inproj(x, g, sc, sh, w_bf, b, tm, tiles_per_mod, per_row):
    t, d = x.shape
    n = w_bf.shape[1]
    tn = d // 8
    return pl.pallas_call(
        _inproj_kernel,
        grid=(t // tm, n // tn),
        in_specs=[pl.BlockSpec((tm, d), lambda m, j: (m, 0)),
                  pl.BlockSpec((1, d), lambda m, j: (0, 0)),
                  _mod_spec(tm, d, tiles_per_mod, per_row),
                  _mod_spec(tm, d, tiles_per_mod, per_row),
                  pl.BlockSpec((d, tn), lambda m, j: (0, j)),
                  pl.BlockSpec((1, tn), lambda m, j: (0, j))],
        out_specs=pl.BlockSpec((tm, tn), lambda m, j: (m, j)),
        out_shape=jax.ShapeDtypeStruct((t, n), F32),
        scratch_shapes=[pltpu.VMEM((tm, d), BF16)],
        compiler_params=_cp("arbitrary", "arbitrary"),
        name="inproj",
    )(x, g, sc, sh, w_bf, b)


def _rope(x, tab_ref):
    outs = []
    for j in range(x.shape[1] // LANES):
        xb = x[:, j * LANES:(j + 1) * LANES]
        outs.append(xb * tab_ref[0]
                    + pltpu.roll(xb, LANES - ROT_DIM // 2, 1) * tab_ref[1]
                    + pltpu.roll(xb, ROT_DIM // 2, 1) * tab_ref[2])
    return outs[0] if len(outs) == 1 else jnp.concatenate(outs, axis=1)


def _attn_kernel(sink_ref, q_ref, kc_ref, vc_ref, kp_ref, vp_ref, tc_ref, tp_ref, mk_ref, g_ref,
                 o_ref, kr_ref, att_ref, *, rope_prev, first_has_no_prev):
    qb, pb = q_ref.shape[0], kp_ref.shape[0]
    nk = pb + qb
    q = (_rope(q_ref[...], tc_ref) * np.float32(HEAD_DIM ** -0.5)).astype(BF16)
    kc = _rope(kc_ref[...], tc_ref)
    kr_ref[...] = kc
    kp = _rope(kp_ref[...], tp_ref) if rope_prev else kp_ref[...]
    k_all = jnp.concatenate([kp, kc], axis=0)
    v_all = jnp.concatenate([vp_ref[...], vc_ref[...]], axis=0)
    mk = mk_ref[...]
    if first_has_no_prev:
        col = lax.broadcasted_iota(I32, (qb, nk), 1)
        mk = jnp.where((col < pb) & (pl.program_id(1) == 0), 0.0, mk)
    mask = mk > 0.5
    low = lax.broadcasted_iota(I32, (nk, LANES), 1) < HEAD_DIM
    low_q = lax.broadcasted_iota(I32, (qb, LANES), 1) < HEAD_DIM
    ones_lo = jnp.where(low, 1.0, 0.0).astype(BF16)
    ones_hi = jnp.where(low, 0.0, 1.0).astype(BF16)
    dn = (((1,), (1,)), ((), ()))
    pairs_per_kv = Q_PER_KV // 2
    for kb in range(k_all.shape[1] // LANES):
        kblk = k_all[:, kb * LANES:(kb + 1) * LANES]
        vblk = v_all[:, kb * LANES:(kb + 1) * LANES]
        kswp = pltpu.roll(kblk, HEAD_DIM, 1)
        vswp = pltpu.roll(vblk, HEAD_DIM, 1)
        for par in range(2):
            k_src, k_oth = (kblk, kswp) if par == 0 else (kswp, kblk)
            v_src, v_oth = (vblk, vswp) if par == 0 else (vswp, vblk)
            k_lo = jnp.where(low, k_src, 0.0).astype(BF16)
            k_hi = jnp.where(low, 0.0, k_oth).astype(BF16)
            v_lo = jnp.where(low, v_src, 0.0).astype(BF16)
            v_hi = jnp.where(low, 0.0, v_oth).astype(BF16)
            kv = 2 * kb + par
            for pr in range(pairs_per_kv):
                blk = kv * pairs_per_kv + pr
                qp = q[:, blk * LANES:(blk + 1) * LANES]
                ps, es = [], []
                for half, k_half in enumerate((k_lo, k_hi)):
                    sink = sink_ref[2 * blk + half]
                    s = jnp.where(mask, lax.dot_general(qp, k_half, dn, preferred_element_type=F32), -jnp.inf)
                    m = jnp.maximum(jnp.max(s, axis=-1, keepdims=True), sink)
                    ps.append(jnp.exp(s - m).astype(BF16))
                    es.append(jnp.exp(sink - m))
                acc = (jnp.dot(ps[0], v_lo, preferred_element_type=F32)
                       + jnp.dot(ps[1], v_hi, preferred_element_type=F32))
                den = (jnp.dot(ps[0], ones_lo, preferred_element_type=F32)
                       + jnp.dot(ps[1], ones_hi, preferred_element_type=F32)
                       + jnp.where(low_q, es[0], es[1]))
                att_ref[:, blk * LANES:(blk + 1) * LANES] = acc / den
    o_ref[...] = (_rms(att_ref[...]) * g_ref[...]).astype(BF16)


def _attention(z, kprev, vprev, tab, mask_p, mask_c, sinks, g, *, n_batch, qb, pb, d_attn,
               prev_from_z, rope_prev):
    t = z.shape[0]
    kw = d_attn // Q_PER_KV
    nq = t // n_batch // qb
    kcol = (z.shape[1] - 2 * kw) // kw
    cur = lambda b, i: (b * nq + i, 0)
    if prev_from_z:
        prev_k = pl.BlockSpec((pb, kw), lambda b, i: (b * nq + jnp.maximum(i - 1, 0), kcol))
        prev_v = pl.BlockSpec((pb, kw), lambda b, i: (b * nq + jnp.maximum(i - 1, 0), kcol + 1))
        prev_t = pl.BlockSpec((3, pb, LANES), lambda b, i: (0, jnp.maximum(i - 1, 0), 0))
    else:
        prev_k = pl.BlockSpec((pb, kw), lambda b, i: (b * nq + i, 0))
        prev_v = pl.BlockSpec((pb, kw), lambda b, i: (b * nq + i, 0))
        prev_t = pl.BlockSpec((3, qb, LANES), lambda b, i: (0, i, 0))
    kern = functools.partial(_attn_kernel, rope_prev=rope_prev, first_has_no_prev=prev_from_z)
    return pl.pallas_call(
        kern,
        grid=(n_batch, nq),
        in_specs=[pl.BlockSpec(memory_space=pltpu.SMEM),
                  pl.BlockSpec((qb, d_attn), cur),
                  pl.BlockSpec((qb, kw), lambda b, i: (b * nq + i, kcol)),
                  pl.BlockSpec((qb, kw), lambda b, i: (b * nq + i, kcol + 1)),
                  prev_k, prev_v,
                  pl.BlockSpec((3, qb, LANES), lambda b, i: (0, i, 0)),
                  prev_t,
                  pl.BlockSpec((qb, pb + qb), lambda b, i: (0, 0)),
                  pl.BlockSpec((1, d_attn), lambda b, i: (0, 0))],
        out_specs=[pl.BlockSpec((qb, d_attn), cur), pl.BlockSpec((qb, kw), cur)],
        out_shape=[jax.ShapeDtypeStruct((t, d_attn), BF16), jax.ShapeDtypeStruct((t, kw), F32)],
        scratch_shapes=[pltpu.VMEM((qb, d_attn), F32)],
        compiler_params=_cp("arbitrary", "arbitrary"),
        name="sink_attention",
    )(sinks, z, z, z, kprev, vprev, tab, tab, jnp.concatenate([mask_p, mask_c], axis=1), g)


def _gmlp_kernel(zu_ref, zv_ref, wm_ref, bsp_ref, lng_ref, lnb_ref, og_ref, *out_and_scratch, emit_vg):
    if emit_vg:
        o_ref, vg_ref, gm_ref = out_and_scratch
    else:
        o_ref, gm_ref = out_and_scratch
    rt, dg = zu_ref.shape
    r = wm_ref.shape[1]
    gw = dg // GMLP_GROUPS
    for c in range(rt // r):
        rows = slice(c * r, (c + 1) * r)
        for g in range(GMLP_GROUPS):
            cols = slice(g * gw, (g + 1) * gw)
            a = _gelu(zv_ref[rows, cols])
            d = a - jnp.mean(a, axis=-1, keepdims=True)
            var = jnp.mean(d * d, axis=-1, keepdims=True)
            vg = d * lax.rsqrt(var + EPS) * lng_ref[:, cols] + lnb_ref[:, cols]
            if emit_vg:
                vg_ref[rows, cols] = vg
            mixed = jnp.dot(wm_ref[g], vg.astype(BF16), preferred_element_type=F32) + bsp_ref[:, cols]
            gm_ref[rows, cols] = _gelu(zu_ref[rows, cols]) * mixed
        o_ref[rows, :] = (_rms(gm_ref[rows, :]) * og_ref[...]).astype(BF16)


def _gmlp(z, wm, bsp, ln_g, ln_b, out_g, *, rt, dg, emit_vg):
    t = z.shape[0]
    r = wm.shape[1]
    out_shape = [jax.ShapeDtypeStruct((t, dg), BF16)]
    out_specs = [pl.BlockSpec((rt, dg), lambda i: (i, 0))]
    if emit_vg:
        out_shape.append(jax.ShapeDtypeStruct((t, dg), F32))
        out_specs.append(pl.BlockSpec((rt, dg), lambda i: (i, 0)))
    res = pl.pallas_call(
        functools.partial(_gmlp_kernel, emit_vg=emit_vg),
        grid=(t // rt,),
        in_specs=[pl.BlockSpec((rt, dg), lambda i: (i, 1)),
                  pl.BlockSpec((rt, dg), lambda i: (i, 2)),
                  pl.BlockSpec((GMLP_GROUPS, r, r), lambda i: (0, 0, 0)),
                  pl.BlockSpec((r, dg), lambda i: (0, 0)),
                  pl.BlockSpec((1, dg), lambda i: (0, 0)),
                  pl.BlockSpec((1, dg), lambda i: (0, 0)),
                  pl.BlockSpec((1, dg), lambda i: (0, 0))],
        out_specs=out_specs,
        out_shape=out_shape,
        scratch_shapes=[pltpu.VMEM((rt, dg), F32)],
        compiler_params=_cp("arbitrary"),
        name="gmlp",
    )(z, z, wm, bsp, ln_g, ln_b, out_g)
    return res if emit_vg else (res[0], None)


def _outproj_kernel(a_ref, m_ref, x_ref, wa_ref, wb_ref, b_ref, g1_ref, o_ref):
    acc = (jnp.dot(a_ref[...], wa_ref[...], preferred_element_type=F32)
           + jnp.dot(m_ref[...], wb_ref[...], preferred_element_type=F32) + b_ref[...])
    o_ref[...] = x_ref[...] + g1_ref[...] * acc


def _outproj(att, gm, x, w_bf, b, g1, tm, tiles_per_mod, per_row):
    t, d = x.shape
    dh = att.shape[1]
    tn = d // 4
    return pl.pallas_call(
        _outproj_kernel,
        grid=(t // tm, d // tn),
        in_specs=[pl.BlockSpec((tm, dh), lambda m, j: (m, 0)),
                  pl.BlockSpec((tm, dh), lambda m, j: (m, 0)),
                  pl.BlockSpec((tm, tn), lambda m, j: (m, j)),
                  pl.BlockSpec((dh, tn), lambda m, j: (0, j)),
                  pl.BlockSpec((dh, tn), lambda m, j: (1, j)),
                  pl.BlockSpec((1, tn), lambda m, j: (0, j)),
                  _mod_spec(tm, tn, tiles_per_mod, per_row, col=True)],
        out_specs=pl.BlockSpec((tm, tn), lambda m, j: (m, j)),
        out_shape=jax.ShapeDtypeStruct((t, d), F32),
        compiler_params=_cp("arbitrary", "arbitrary"),
        name="outproj",
    )(att, gm, x, w_bf, w_bf, b, g1)


def _router_kernel(x_ref, g_ref, sc_ref, sh_ref, whi_ref, wlo_ref, br_ref, *rest):
    h_ref, idx_ref, gate_ref, hi_ref, lo_ref = rest[-5:]
    per_row = sc_ref.shape[0] != 1
    tm = x_ref.shape[0]

    def step(rows):
        sc = sc_ref[rows, :] if per_row else sc_ref[...]
        sh = sh_ref[rows, :] if per_row else sh_ref[...]
        h = _rms(x_ref[rows, :]) * g_ref[...] * (1.0 + sc) + sh
        h_ref[rows, :] = h
        hi = h.astype(BF16)
        hi_ref[rows, :] = hi
        lo_ref[rows, :] = (h - hi.astype(F32)).astype(BF16)
    _row_loop(tm, step)

    logits = (jnp.dot(hi_ref[...], whi_ref[...], preferred_element_type=F32)
              + jnp.dot(lo_ref[...], whi_ref[...], preferred_element_type=F32)
              + jnp.dot(hi_ref[...], wlo_ref[...], preferred_element_type=F32)) + br_ref[...]
    lane = lax.broadcasted_iota(I32, (tm, LANES), 1).astype(F32)
    l = jnp.where(lane < N_EXPERTS, logits, -jnp.inf)
    vals, idxs = [], []
    for _ in range(TOP_K):
        m = jnp.max(l, axis=-1, keepdims=True)
        idx = jnp.min(jnp.where(l == m, lane, float(LANES)), axis=-1, keepdims=True)
        vals.append(m)
        idxs.append(idx)
        l = jnp.where(lane == idx, -jnp.inf, l)
    es = [jnp.exp(v - vals[0]) for v in vals]
    den = es[0] + es[1] + es[2] + es[3]
    idx_out = jnp.zeros((tm, LANES), I32)
    gate_out = jnp.zeros((tm, LANES), F32)
    for k in range(TOP_K):
        idx_out = jnp.where(lane == k, idxs[k].astype(I32), idx_out)
        gate_out = jnp.where(lane == k, es[k] / den, gate_out)
    idx_ref[...] = idx_out
    gate_ref[...] = gate_out


def _router(x1, g, sc, sh, whi, wlo, br, tm, tiles_per_mod, per_row, t_all, tile_off=0, buffers=()):
    t, d = x1.shape
    row = lambda m: (m, 0)
    out_row = lambda m: (m + tile_off, 0)
    mod = _mod_spec_1d(tm, d, tiles_per_mod, per_row)
    n_in = 7
    return pl.pallas_call(
        _router_kernel,
        grid=(t // tm,),
        in_specs=[pl.BlockSpec((tm, d), row),
                  pl.BlockSpec((1, d), lambda m: (0, 0)),
                  mod, mod,
                  pl.BlockSpec((d, LANES), lambda m: (0, 0)),
                  pl.BlockSpec((d, LANES), lambda m: (0, 0)),
                  pl.BlockSpec((1, LANES), lambda m: (0, 0))]
                 + [pl.BlockSpec(memory_space=pl.ANY)] * len(buffers),
        out_specs=[pl.BlockSpec((tm, d), out_row), pl.BlockSpec((tm, LANES), out_row),
                   pl.BlockSpec((tm, LANES), out_row)],
        out_shape=[jax.ShapeDtypeStruct((t_all, d), F32), jax.ShapeDtypeStruct((t_all, LANES), I32),
                   jax.ShapeDtypeStruct((t_all, LANES), F32)],
        input_output_aliases={n_in + k: k for k in range(len(buffers))},
        scratch_shapes=[pltpu.VMEM((tm, d), BF16), pltpu.VMEM((tm, d), BF16)],
        compiler_params=_cp("arbitrary"),
        name="router_top4",
    )(x1, g, sc, sh, whi, wlo, br, *buffers)


def _plan_kernel(idx_ref, dest_ref, cnt_ref, cnt_scr, carry_scr, start_scr, *, tile_rows):
    p = pl.program_id(0)
    i = pl.program_id(1)
    tp = idx_ref.shape[0]
    lane = lax.broadcasted_iota(I32, (tp, LANES), 1)
    idx = idx_ref[...]
    onehot = [idx[:, k:k + 1] == lane for k in range(TOP_K)]
    hist = sum(jnp.where(o, 1.0, 0.0) for o in onehot)
    col_sum = jnp.sum(hist, axis=0, keepdims=True)

    @pl.when((p == 0) & (i == 0))
    def _():
        cnt_scr[...] = jnp.zeros_like(cnt_scr)

    @pl.when(p == 0)
    def _():
        cnt_scr[...] += col_sum

    @pl.when((p == 1) & (i == 0))
    def _():
        cnt = cnt_scr[...]
        cnt_ref[...] = cnt
        tiles = jnp.floor((cnt + (tile_rows - 1)) * (1.0 / tile_rows))
        before = (lax.broadcasted_iota(I32, (LANES, LANES), 0)
                  < lax.broadcasted_iota(I32, (LANES, LANES), 1))
        tri = jnp.where(before, 1.0, 0.0).astype(BF16)
        start = jnp.dot(jnp.broadcast_to(tiles, (8, LANES)).astype(BF16), tri, preferred_element_type=F32)
        start_scr[...] = start[0:1] * tile_rows
        carry_scr[...] = jnp.zeros_like(carry_scr)

    @pl.when(p == 1)
    def _():
        earlier = (lax.broadcasted_iota(I32, (tp, tp), 0) > lax.broadcasted_iota(I32, (tp, tp), 1))
        low = jnp.where(earlier, 1.0, 0.0).astype(BF16)
        pos = (jnp.dot(low, hist.astype(BF16), preferred_element_type=F32)
               + carry_scr[...] + start_scr[...])
        dest = jnp.zeros((tp, LANES), I32)
        for k in range(TOP_K):
            d = jnp.sum(jnp.where(onehot[k], pos, 0.0), axis=-1, keepdims=True)
            dest = jnp.where(lane == k, d.astype(I32), dest)
        dest_ref[...] = dest
        carry_scr[...] += col_sum


def _plan(idx_all, tile_rows):
    t = idx_all.shape[0]
    tp = 256
    return pl.pallas_call(
        functools.partial(_plan_kernel, tile_rows=tile_rows),
        grid=(2, t // tp),
        in_specs=[pl.BlockSpec((tp, LANES), lambda p, i: (i, 0))],
        out_specs=[pl.BlockSpec((tp, LANES), lambda p, i: (i * p, 0)),
                   pl.BlockSpec((1, LANES), lambda p, i: (0, 0))],
        out_shape=[jax.ShapeDtypeStruct((t, LANES), I32), jax.ShapeDtypeStruct((1, LANES), F32)],
        scratch_shapes=[pltpu.VMEM((1, LANES), F32), pltpu.VMEM((1, LANES), F32),
                        pltpu.VMEM((1, LANES), F32)],
        compiler_params=_cp("arbitrary", "arbitrary"),
        name="dispatch_plan",
    )(idx_all)


DMA_UNROLL = 8


def _gather_kernel(nu_ref, tok_ref, nxt_ref, h_ref, o_ref, buf_ref, sem):
    i = pl.program_id(0)
    nu = nu_ref[0]
    gt = o_ref.shape[0]
    slot = i % 2

    def copy(tok, r, s):
        return pltpu.make_async_copy(h_ref.at[pl.ds(tok[0, 0, r], 1), :], buf_ref.at[s, pl.ds(r, 1), :],
                                     sem.at[s])

    def issue(tok, s):
        def body(r, c):
            copy(tok, r, s).start()
            return c
        lax.fori_loop(0, gt, body, 0, unroll=DMA_UNROLL)

    @pl.when(i == 0)
    def _():
        issue(tok_ref, 0)

    @pl.when(i + 1 < nu)
    def _():
        issue(nxt_ref, 1 - slot)

    @pl.when(i < nu)
    def _():
        def body(r, c):
            copy(tok_ref, r, slot).wait()
            return c
        lax.fori_loop(0, gt, body, 0, unroll=DMA_UNROLL)
        o_ref[...] = buf_ref[slot].astype(BF16)


def _gather(n_used_rows, slot_tok, h_all, n_slots, gt):
    d = h_all.shape[1]
    nt = n_slots // gt
    tok3 = slot_tok.reshape(nt, 1, gt)
    return pl.pallas_call(
        _gather_kernel,
        grid_spec=pltpu.PrefetchScalarGridSpec(
            num_scalar_prefetch=1,
            grid=(nt,),
            in_specs=[pl.BlockSpec((1, 1, gt), lambda i, nu: (jnp.minimum(i, nu[0] - 1), 0, 0),
                                   memory_space=pltpu.SMEM),
                      pl.BlockSpec((1, 1, gt), lambda i, nu: (jnp.minimum(i + 1, nu[0] - 1), 0, 0),
                                   memory_space=pltpu.SMEM),
                      pl.BlockSpec(memory_space=pl.ANY)],
            out_specs=pl.BlockSpec((gt, d), lambda i, nu: (jnp.minimum(i, nu[0] - 1), 0)),
            scratch_shapes=[pltpu.VMEM((2, gt, d), F32), pltpu.SemaphoreType.DMA((2,))]),
        out_shape=jax.ShapeDtypeStruct((n_slots, d), BF16),
        compiler_params=_cp("arbitrary"),
        name="moe_gather",
    )(n_used_rows, tok3, tok3, h_all)


CAST_SLAB = 256


def _stage_expert_weights(n, i, n_pass, gidx_ref, gexp_ref, ng_ref, copies, stage_ref, w_scr):
    g = gidx_ref[i]

    @pl.when((n == 0) & (i == 0))
    def _():
        for c in copies(gexp_ref[0], 0):
            c.start()

    for c in copies(gexp_ref[g], n):
        c.wait()

    def body(r, c):
        rows = pl.ds(pl.multiple_of(r * CAST_SLAB, CAST_SLAB), CAST_SLAB)
        w_scr[rows, :] = stage_ref[rows, :].astype(BF16)
        return c
    lax.fori_loop(0, stage_ref.shape[0] // CAST_SLAB, body, 0)

    wraps = g + 1 >= ng_ref[0]
    g_next = jnp.where(wraps, 0, g + 1)
    n_next = jnp.where(wraps, n + 1, n)

    @pl.when(n_next < n_pass)
    def _():
        for c in copies(gexp_ref[g_next], n_next):
            c.start()


def _new_expert(te_ref, i):
    return (i == 0) | (te_ref[i] != te_ref[jnp.maximum(i - 1, 0)])


def _moe_up_kernel(te_ref, nu_ref, gidx_ref, gexp_ref, ng_ref, x_ref, w_hbm, bg_ref, bu_ref, o_ref,
                   stage_ref, w_scr, sem):
    n, i = pl.program_id(0), pl.program_id(1)
    tn = o_ref.shape[1]
    f = w_hbm.shape[2] // 2
    used = i < nu_ref[0]

    def copies(e, nn):
        c0 = pl.multiple_of(nn * tn, tn)
        return (pltpu.make_async_copy(w_hbm.at[e, :, pl.ds(c0, tn)], stage_ref.at[:, pl.ds(0, tn)], sem.at[0]),
                pltpu.make_async_copy(w_hbm.at[e, :, pl.ds(f + c0, tn)], stage_ref.at[:, pl.ds(tn, tn)],
                                      sem.at[1]))

    @pl.when(used & _new_expert(te_ref, i))
    def _():
        _stage_expert_weights(n, i, f // tn, gidx_ref, gexp_ref, ng_ref, copies, stage_ref, w_scr)

    @pl.when(used)
    def _():
        gu = jnp.dot(x_ref[...], w_scr[...], preferred_element_type=F32)
        g = jnp.minimum(gu[:, :tn] + bg_ref[0], SWIGLU_LIMIT)
        u = jnp.clip(gu[:, tn:] + bu_ref[0], -SWIGLU_LIMIT, SWIGLU_LIMIT)
        o_ref[...] = ((u + 1.0) * g * jax.nn.sigmoid(SWIGLU_ALPHA * g)).astype(BF16)


def _moe_down_kernel(te_ref, nu_ref, gidx_ref, gexp_ref, ng_ref, a_ref, w_hbm, b_ref, o_ref,
                     stage_ref, w_scr, sem):
    n, i = pl.program_id(0), pl.program_id(1)
    tn = o_ref.shape[1]
    used = i < nu_ref[0]

    def copies(e, nn):
        c0 = pl.multiple_of(nn * tn, tn)
        return (pltpu.make_async_copy(w_hbm.at[e, :, pl.ds(c0, tn)], stage_ref, sem.at[0]),)

    @pl.when(used & _new_expert(te_ref, i))
    def _():
        _stage_expert_weights(n, i, w_hbm.shape[2] // tn, gidx_ref, gexp_ref, ng_ref, copies, stage_ref, w_scr)

    @pl.when(used)
    def _():
        o_ref[...] = jnp.dot(a_ref[...], w_scr[...], preferred_element_type=F32) + b_ref[0]


def _moe_up(plan, xs, w_gate_up, b_gate_up, tm):
    s, d = xs.shape
    f = w_gate_up.shape[2] // 2
    tn = f // 8
    nb = f // tn
    tile = lambda n, i, te, nu, *_: jnp.minimum(i, nu[0] - 1)
    expert = lambda n, i, te, nu, *_: te[jnp.minimum(i, nu[0] - 1)]
    b3 = b_gate_up.reshape(N_EXPERTS, 1, 2 * f)
    return pl.pallas_call(
        _moe_up_kernel,
        grid_spec=pltpu.PrefetchScalarGridSpec(
            num_scalar_prefetch=5,
            grid=(nb, s // tm),
            in_specs=[pl.BlockSpec((tm, d), lambda n, i, *p: (tile(n, i, *p), 0)),
                      pl.BlockSpec(memory_space=pl.ANY),
                      pl.BlockSpec((1, 1, tn), lambda n, i, *p: (expert(n, i, *p), 0, n)),
                      pl.BlockSpec((1, 1, tn), lambda n, i, *p: (expert(n, i, *p), 0, n + nb))],
            out_specs=pl.BlockSpec((tm, tn), lambda n, i, *p: (tile(n, i, *p), n)),
            scratch_shapes=[pltpu.VMEM((d, 2 * tn), F32), pltpu.VMEM((d, 2 * tn), BF16),
                            pltpu.SemaphoreType.DMA((2,))]),
        out_shape=jax.ShapeDtypeStruct((s, f), BF16),
        compiler_params=_cp("arbitrary", "arbitrary"),
        name="moe_gate_up",
    )(*plan, xs, w_gate_up, b3, b3)


def _moe_down(plan, act, w_down, b_down, tm):
    s, f = act.shape
    d = w_down.shape[2]
    tn = d // 4
    tile = lambda n, i, te, nu, *_: jnp.minimum(i, nu[0] - 1)
    expert = lambda n, i, te, nu, *_: te[jnp.minimum(i, nu[0] - 1)]
    return pl.pallas_call(
        _moe_down_kernel,
        grid_spec=pltpu.PrefetchScalarGridSpec(
            num_scalar_prefetch=5,
            grid=(d // tn, s // tm),
            in_specs=[pl.BlockSpec((tm, f), lambda n, i, *p: (tile(n, i, *p), 0)),
                      pl.BlockSpec(memory_space=pl.ANY),
                      pl.BlockSpec((1, 1, tn), lambda n, i, *p: (expert(n, i, *p), 0, n))],
            out_specs=pl.BlockSpec((tm, tn), lambda n, i, *p: (tile(n, i, *p), n)),
            scratch_shapes=[pltpu.VMEM((f, tn), F32), pltpu.VMEM((f, tn), BF16),
                            pltpu.SemaphoreType.DMA((1,))]),
        out_shape=jax.ShapeDtypeStruct((s, d), F32),
        compiler_params=_cp("arbitrary", "arbitrary"),
        name="moe_down",
    )(*plan, act, w_down, b_down.reshape(N_EXPERTS, 1, d))


def _combine_kernel(dest_ref, nxt_ref, gate_ref, y_ref, x_ref, g2_ref, fg_ref, o_ref, buf_ref, sem):
    ct = x_ref.shape[0]
    per_row = g2_ref.shape[0] != 1
    i = pl.program_id(0)
    slot = i % 2

    def copy(dref, k, r, s):
        return pltpu.make_async_copy(y_ref.at[pl.ds(dref[0, 0, k * ct + r], 1), :],
                                     buf_ref.at[s, k, pl.ds(r, 1), :], sem.at[s])

    def issue(dref, s):
        for k in range(TOP_K):
            def body(r, c, k=k):
                copy(dref, k, r, s).start()
                return c
            lax.fori_loop(0, ct, body, 0, unroll=DMA_UNROLL)

    @pl.when(i == 0)
    def _():
        issue(dest_ref, 0)

    @pl.when(i + 1 < pl.num_programs(0))
    def _():
        issue(nxt_ref, 1 - slot)

    for k in range(TOP_K):
        def body(r, c, k=k):
            copy(dest_ref, k, r, slot).wait()
            return c
        lax.fori_loop(0, ct, body, 0, unroll=DMA_UNROLL)

    def step(rows):
        gate = gate_ref[rows, :]
        ff = gate[:, 0:1] * buf_ref[slot, 0, rows, :]
        for k in range(1, TOP_K):
            ff = ff + gate[:, k:k + 1] * buf_ref[slot, k, rows, :]
        g2 = g2_ref[rows, :] if per_row else g2_ref[...]
        o_ref[rows, :] = _rms(x_ref[rows, :] + g2 * ff) * fg_ref[...]
    _row_loop(ct, step)


def _combine(dest3, gate_all, y, x1, g2, fg, tile_off, ct, tiles_per_mod, per_row):
    t, d = x1.shape
    nt = t // ct
    row = lambda i: (i, 0)
    mod = _mod_spec_1d(ct, d, tiles_per_mod, per_row)
    return pl.pallas_call(
        _combine_kernel,
        grid=(nt,),
        in_specs=[pl.BlockSpec((1, 1, ct * TOP_K), lambda i: (i + tile_off, 0, 0), memory_space=pltpu.SMEM),
                  pl.BlockSpec((1, 1, ct * TOP_K), lambda i: (jnp.minimum(i + 1, nt - 1) + tile_off, 0, 0),
                               memory_space=pltpu.SMEM),
                  pl.BlockSpec((ct, LANES), lambda i: (i + tile_off, 0)),
                  pl.BlockSpec(memory_space=pl.ANY),
                  pl.BlockSpec((ct, d), row),
                  mod,
                  pl.BlockSpec((1, d), lambda i: (0, 0))],
        out_specs=pl.BlockSpec((ct, d), row),
        out_shape=jax.ShapeDtypeStruct((t, d), F32),
        scratch_shapes=[pltpu.VMEM((2, TOP_K, ct, d), F32), pltpu.SemaphoreType.DMA((2,))],
        compiler_params=_cp("arbitrary"),
        name="moe_combine",
    )(dest3, dest3, gate_all, y, x1, g2, fg)


def _rope_tables(pos):
    half = ROT_DIM // 2
    inv = ROPE_THETA ** (-jnp.arange(0, ROT_DIM, 2, dtype=F32) / ROT_DIM)
    ang = pos[:, None] * inv[None, :]
    cos, sin = jnp.cos(ang), jnp.sin(ang)
    m = np.arange(LANES) % HEAD_DIM
    sel = m % half
    c = jnp.where(m < ROT_DIM, cos[:, sel], 1.0)
    s1 = jnp.where(m < half, -sin[:, sel], 0.0)
    s2 = jnp.where((m >= half) & (m < ROT_DIM), sin[:, sel], 0.0)
    return jnp.stack([c, s1, s2]).astype(F32)


def _trunk_to_x1(x2d, mod_rows, per_row, tm, tiles_per_mod, attn_fn, gmlp_fn, p):
    sh1, sc1, g1 = mod_rows[0], mod_rows[1], mod_rows[2]
    z = _inproj(x2d, p["norm_mix_g"], sc1, sh1, p["w_in"], p["b_in"], tm, tiles_per_mod, per_row)
    att, k_rope = attn_fn(z)
    gm, vg = gmlp_fn(z)
    x1 = _outproj(att, gm, x2d, p["w_out"], p["b_out"], g1, tm, tiles_per_mod, per_row)
    return x1, k_rope, z, vg


def kernel(x_prompt, x_sample, c_prompt, c_sample, cache_k, cache_v, w_ada, b_ada, norm_mix_g, w_in, b_in,
           sinks, ln_v_g, ln_v_b, w_spatial, b_spatial, attn_out_g, gmlp_out_g, w_out, b_out, norm_ffn_g,
           w_router, b_router, w_gate_up, b_gate_up, w_down, b_down, final_norm_g):
    nb, seq, d = x_prompt.shape
    ns, sseq, _ = x_sample.shape
    da = d // 2
    dg = d - da
    kw = da // Q_PER_KV
    n_kv = kw // HEAD_DIM
    gw = dg // GMLP_GROUPS
    tp, ts = nb * seq, ns * sseq
    l = 0

    n_mod = nb + ns
    pad = -n_mod % 16
    c_all = jnp.concatenate([c_prompt, c_sample, jnp.zeros((pad, d), F32)], axis=0)
    mod = _ada(c_all, w_ada[l], b_ada[l])
    mod_p = [m.reshape(nb, 1, d) for m in jnp.split(mod[:nb], 6, axis=-1)]
    mod_s = [jnp.repeat(m, sseq, axis=0) for m in jnp.split(mod[nb:n_mod], 6, axis=-1)]

    o1, o2, o3, o4 = da, da + kw, da + 2 * kw, da + 2 * kw + dg
    perm = np.concatenate([np.arange(0, o1), np.arange(o3, o4), np.arange(o4, o4 + dg),
                           np.arange(o1, o2), np.arange(o2, o3)])
    row1 = lambda v: v.reshape(1, -1)
    p = {
        "norm_mix_g": row1(norm_mix_g[l]),
        "w_in": w_in[l][:, perm].astype(BF16),
        "b_in": row1(b_in[l][perm]),
        "w_out": w_out[l].astype(BF16),
        "b_out": row1(b_out[l]),
    }
    cmask = (np.arange(GMLP_CHUNK)[:, None] // CHUNK) >= (np.arange(GMLP_CHUNK)[None, :] // CHUNK)
    w_sp = w_spatial[l] * jnp.asarray(cmask, F32)[None]
    wm_p = w_sp.astype(BF16)
    bsp_p = jnp.repeat(b_spatial[l].T, gw, axis=1)
    eye = jnp.eye(ns, dtype=F32)
    wm_s = jnp.einsum("ab,gij->gaibj", eye, w_sp[:, :sseq, :sseq]).reshape(
        GMLP_GROUPS, ts, ts).astype(BF16)
    bsp_s = jnp.tile(bsp_p[:sseq], (ns, 1))
    sink_v = sinks[l]
    ag, gg = row1(attn_out_g[l]), row1(gmlp_out_g[l])
    lng, lnb = row1(ln_v_g[l]), row1(ln_v_b[l])

    qb = WINDOW
    tab_p = _rope_tables(jnp.arange(seq, dtype=F32))
    qi = np.arange(qb) // CHUNK
    mask_prev = jnp.asarray(qi[None, :] >= qi[:, None], F32)
    mask_cur = jnp.asarray(qi[None, :] <= qi[:, None], F32)
    tm_p = 512
    attn_p = lambda z: _attention(z, z, z, tab_p, mask_prev, mask_cur, sink_v, ag, n_batch=nb, qb=qb, pb=qb,
                                  d_attn=da, prev_from_z=True, rope_prev=True)
    gmlp_p = lambda z: _gmlp(z, wm_p, bsp_p, lng, lnb, gg, rt=4 * GMLP_CHUNK, dg=dg, emit_vg=False)
    x1_p, k_p, z_p, _ = _trunk_to_x1(x_prompt.reshape(tp, d), mod_p, False, tm_p, seq // tm_p,
                                     attn_p, gmlp_p, p)

    tab_s = _rope_tables(jnp.tile(PAST_LEN + jnp.arange(sseq, dtype=F32), ns))
    n_cache = cache_k.shape[2]
    sid = np.arange(ts) // sseq
    mask_cache = jnp.asarray(sid[:, None] == (np.arange(ns * n_cache) // n_cache)[None, :], F32)
    mask_new = jnp.asarray(sid[:, None] == sid[None, :], F32)
    ck = cache_k[l].reshape(ns * n_cache, kw)
    cv = cache_v[l].reshape(ns * n_cache, kw)
    attn_s = lambda z: _attention(z, ck, cv, tab_s, mask_cache, mask_new, sink_v, ag, n_batch=1, qb=ts,
                                  pb=ns * n_cache, d_attn=da, prev_from_z=False, rope_prev=False)
    gmlp_s = lambda z: _gmlp(z, wm_s, bsp_s, lng, lnb, gg, rt=ts, dg=dg, emit_vg=True)
    x1_s, k_s, z_s, vg_s = _trunk_to_x1(x_sample.reshape(ts, d), mod_s, True, ts, 1, attn_s, gmlp_s, p)

    w_r = jnp.pad(w_router[l], ((0, 0), (0, LANES - N_EXPERTS)))
    whi = w_r.astype(BF16)
    wlo = (w_r - whi.astype(F32)).astype(BF16)
    br = jnp.pad(b_router[l], (0, LANES - N_EXPERTS)).reshape(1, LANES)
    fng = row1(norm_ffn_g[l])
    tm_r = 256
    t_all = tp + ts
    bufs = _router(x1_p, fng, mod_p[4], mod_p[3], whi, wlo, br, tm_r, seq // tm_r, False, t_all)
    h_all, idx_all, gate_all = _router(x1_s, fng, mod_s[4], mod_s[3], whi, wlo, br, ts, 1, True, t_all,
                                       tile_off=tp // ts, buffers=tuple(bufs))
    n_assign = t_all * TOP_K
    tm = MOE_TM
    n_tiles = n_assign // tm + N_EXPERTS
    n_slots = n_tiles * tm

    dest_pad, cnt = _plan(idx_all, tm)
    dest = dest_pad[:, :TOP_K]
    counts = cnt[0, :N_EXPERTS].astype(I32)
    tile_end = jnp.cumsum((counts + tm - 1) // tm)
    n_used = tile_end[-1:].astype(I32)
    tile_expert = jnp.minimum(jnp.sum(jnp.arange(n_tiles)[:, None] >= tile_end[None, :], axis=1),
                              N_EXPERTS - 1).astype(I32)
    present = counts > 0
    group_expert = jnp.argsort(jnp.logical_not(present), stable=True).astype(I32)
    n_groups = jnp.sum(present).astype(I32).reshape(1)
    tile_group = (jnp.cumsum(present) - 1)[tile_expert].astype(I32)
    plan = (tile_expert, n_used, tile_group, group_expert, n_groups)
    slot_tok = jnp.zeros((n_slots,), I32).at[dest.reshape(-1)].set(
        jnp.repeat(jnp.arange(t_all, dtype=I32), TOP_K))

    gt = 256
    xs = _gather(n_used * (tm // gt), slot_tok, h_all, n_slots, gt)
    act = _moe_up(plan, xs, w_gate_up[l], b_gate_up[l], tm)
    y = _moe_down(plan, act, w_down[l], b_down[l], tm)

    ct = 128
    dest3 = dest.reshape(t_all // ct, ct, TOP_K).transpose(0, 2, 1).reshape(t_all // ct, 1, TOP_K * ct)
    fg = row1(final_norm_g)
    y_p = _combine(dest3, gate_all, y, x1_p, mod_p[5], fg, 0, ct, seq // ct, False)
    y_s = _combine(dest3, gate_all, y, x1_s, mod_s[5], fg, tp // ct, ct, 1, True)

    vcol = z_p.shape[1] - kw
    new_k_p = k_p.reshape(nb, seq, n_kv, HEAD_DIM)[:, -WINDOW:][None]
    new_v_p = z_p[:, vcol:].reshape(nb, seq, n_kv, HEAD_DIM)[:, -WINDOW:][None]
    new_k_s = k_s.reshape(ns, sseq, n_kv, HEAD_DIM)[None]
    new_v_s = z_s[:, vcol:].reshape(ns, sseq, n_kv, HEAD_DIM)[None]
    new_vg_s = vg_s.reshape(ns, sseq, dg)[None]
    return (y_p.reshape(nb, seq, d), y_s.reshape(ns, sseq, d), new_k_p, new_v_p, new_k_s, new_v_s, new_vg_s)
```
